```python
import math
import jax
import jax.numpy as jnp
from jax import lax
import numpy as np

D_MODEL = 2048
BATCH = 4
SEQ = 2048
DEPTH = 2
DEC_BATCH = 128
DEC_SEQ = 4
PAST_LEN = 16384
PAGE_SIZE = 128

MIX_WIDTH = D_MODEL
ML_WIDTH = MIX_WIDTH // 2
POOL_WIDTH = MIX_WIDTH - ML_WIDTH
ML_HEADS = 4
ML_DK = ML_WIDTH // ML_HEADS
ML_DV = ML_WIDTH // ML_HEADS
MLSTM_CHUNK = 64
POOL_WINDOWS = (2, 4, 8, 16)
POOL_GROUPS = len(POOL_WINDOWS)
POOL_GW = POOL_WIDTH // POOL_GROUPS
POOL_BUF = max(POOL_WINDOWS) - 1
IN_COLS = 4 * ML_WIDTH + 2 * ML_HEADS + POOL_WIDTH
N_EXPERTS = 32
TOP_K = 4
D_FF = D_MODEL
SWIGLU_LIMIT = 7.0
SWIGLU_ALPHA = 1.702
MOE_BLOCK = 128
N_MOD = 6
EPS = 1e-5

kernel_name = "hymba_mlstm_pool_moe_adaln_step"


def rmsnorm(x, g):
    xf = x.astype(jnp.float32)
    y = xf * lax.rsqrt(jnp.mean(xf * xf, axis=-1, keepdims=True) + EPS)
    return (y * g.astype(jnp.float32)).astype(x.dtype)


def mlstm_chunkwise(q, k, v, ig, lf, C0, n0, m0, chunk):
    B, H, S, DK = q.shape
    DV = v.shape[-1]
    nc = S // chunk

    def to_chunks(a):
        return jnp.moveaxis(a.reshape(B, H, nc, chunk, *a.shape[3:]), 2, 0)

    causal = jnp.tril(jnp.ones((chunk, chunk), dtype=bool))

    def step(carry, xs):
        C, n, m = carry
        qc, kc, vc, igc, lfc = xs
        b = jnp.cumsum(lfc, axis=-1)
        logw = jnp.where(causal, b[..., :, None] - b[..., None, :] + igc[..., None, :], -jnp.inf)
        inter = b + m[..., None]
        m_t = jnp.maximum(inter, jnp.max(logw, axis=-1))
        s = jnp.einsum('bhtd,bhsd->bhts', qc, kc) * jnp.exp(logw - m_t[..., None])
        a = jnp.exp(inter - m_t)
        num = a[..., None] * jnp.einsum('bhtd,bhde->bhte', qc, C) + jnp.einsum('bhts,bhse->bhte', s, vc)
        den = a * jnp.einsum('bhtd,bhd->bht', qc, n) + jnp.sum(s, axis=-1)
        hc = num / jnp.maximum(jnp.abs(den), jnp.exp(-m_t))[..., None]
        b_end = b[..., -1]
        logw_end = b_end[..., None] - b + igc
        m_new = jnp.maximum(b_end + m, jnp.max(logw_end, axis=-1))
        decay = jnp.exp(b_end + m - m_new)
        wk = kc * jnp.exp(logw_end - m_new[..., None])[..., None]
        C_new = decay[..., None, None] * C + jnp.einsum('bhsd,bhse->bhde', wk, vc)
        n_new = decay[..., None] * n + jnp.sum(wk, axis=2)
        return (C_new, n_new, m_new), hc

    (C, n, m), hs = lax.scan(step, (C0, n0, m0),
                             (to_chunks(q), to_chunks(k), to_chunks(v), to_chunks(ig), to_chunks(lf)))
    h = jnp.moveaxis(hs, 0, 2).reshape(B, H, S, DV)
    return h, C, n, m


def pool_mix(u, buf, n_past, w_pool, ls_pool):
    B, S, _ = u.shape
    u_ext = jnp.concatenate([buf.astype(u.dtype), u], axis=1)
    cs = jnp.cumsum(u_ext.astype(jnp.float32), axis=1)
    cs = jnp.concatenate([jnp.zeros((B, 1, POOL_WIDTH), jnp.float32), cs], axis=1)
    t = jnp.arange(S)
    uf = u.astype(jnp.float32)
    outs = []
    for g, w in enumerate(POOL_WINDOWS):
        sl = slice(g * POOL_GW, (g + 1) * POOL_GW)
        win_sum = cs[:, POOL_BUF + 1:POOL_BUF + 1 + S, sl] - cs[:, POOL_BUF + 1 - w:POOL_BUF + 1 - w + S, sl]
        cnt = jnp.minimum(n_past + t + 1, w).astype(jnp.float32)
        outs.append(win_sum / cnt[None, :, None] - uf[..., sl])
    d = jnp.stack(outs, axis=2)
    p = jnp.einsum('bsgc,gcd->bsgd', d, w_pool.astype(jnp.float32)).reshape(B, S, POOL_WIDTH)
    p = p * ls_pool.astype(jnp.float32)
    return p.astype(u.dtype), u_ext[:, -POOL_BUF:, :]


def mixer(h, n_past, C0, n0, m0, buf0, w_in, b_gates, g_head, w_pool, ls_pool, w_out):
    B, S, _ = h.shape
    proj = h @ w_in
    q = proj[..., 0:ML_WIDTH]
    k = proj[..., ML_WIDTH:2 * ML_WIDTH]
    v = proj[..., 2 * ML_WIDTH:3 * ML_WIDTH]
    og = proj[..., 3 * ML_WIDTH:4 * ML_WIDTH]
    gates = proj[..., 4 * ML_WIDTH:4 * ML_WIDTH + 2 * ML_HEADS].astype(jnp.float32) + b_gates.astype(jnp.float32)
    u = proj[..., 4 * ML_WIDTH + 2 * ML_HEADS:]

    def heads(a, d):
        return a.reshape(B, S, ML_HEADS, d).transpose(0, 2, 1, 3).astype(jnp.float32)

    ig = gates[..., :ML_HEADS].transpose(0, 2, 1)
    lf = jax.nn.log_sigmoid(gates[..., ML_HEADS:]).transpose(0, 2, 1)
    hm, C, n, m = mlstm_chunkwise(heads(q, ML_DK), heads(k, ML_DK) * (ML_DK ** -0.5), heads(v, ML_DV),
                                  ig, lf, C0.astype(jnp.float32), n0.astype(jnp.float32),
                                  m0.astype(jnp.float32), math.gcd(S, MLSTM_CHUNK))
    hm = hm.transpose(0, 2, 1, 3)
    hm = hm * lax.rsqrt(jnp.mean(hm * hm, axis=-1, keepdims=True) + EPS) \
        * g_head.reshape(ML_HEADS, ML_DV).astype(jnp.float32)
    hm = hm.reshape(B, S, ML_WIDTH) * jax.nn.sigmoid(og.astype(jnp.float32))
    p, buf = pool_mix(u, buf0, n_past, w_pool, ls_pool)
    y = jnp.concatenate([hm.astype(h.dtype), p], axis=-1) @ w_out
    return y, C, n, m, buf


def moe(h, w_router, b_router, w_gate_up, b_gate_up, w_down, b_down):
    B, S, D = h.shape
    N = B * S
    A = N * TOP_K
    xf = h.reshape(N, D)
    logits = (xf @ w_router + b_router).astype(jnp.float32)
    top_logit, top_idx = lax.top_k(logits, TOP_K)
    gates = jax.nn.softmax(top_logit, axis=-1)
    expert_of = top_idx.reshape(A)
    token_of = jnp.repeat(jnp.arange(N, dtype=jnp.int32), TOP_K)
    order = jnp.argsort(expert_of)
    sorted_e = expert_of[order]
    sorted_tok = token_of[order]
    sorted_gate = gates.reshape(A)[order]
    counts = jnp.bincount(expert_of, length=N_EXPERTS)
    padded = (counts + MOE_BLOCK - 1) // MOE_BLOCK * MOE_BLOCK
    start = jnp.cumsum(counts) - counts
    pend = jnp.cumsum(padded)
    pstart = pend - padded
    dest = pstart[sorted_e] + jnp.arange(A) - start[sorted_e]
    n_blocks = -(-(A + N_EXPERTS * (MOE_BLOCK - 1)) // MOE_BLOCK)
    x_pad = jnp.zeros((n_blocks * MOE_BLOCK, D), h.dtype).at[dest].set(xf[sorted_tok])
    block_expert = jnp.minimum(
        jnp.searchsorted(pend, jnp.arange(n_blocks) * MOE_BLOCK, side='right'), N_EXPERTS - 1)

    def expert_block(args):
        xb, e = args
        gu = xb @ w_gate_up[e] + b_gate_up[e]
        g = jnp.minimum(gu[:, :D_FF], SWIGLU_LIMIT)
        up = jnp.clip(gu[:, D_FF:], -SWIGLU_LIMIT, SWIGLU_LIMIT)
        act = g * jax.nn.sigmoid(SWIGLU_ALPHA * g) * (up + 1)
        return act @ w_down[e] + b_down[e]

    y_pad = lax.map(expert_block, (x_pad.reshape(n_blocks, MOE_BLOCK, D), block_expert)).reshape(-1, D)
    y = jnp.zeros((N, D), h.dtype).at[sorted_tok].add(y_pad[dest] * sorted_gate[:, None].astype(h.dtype))
    return y.reshape(B, S, D)


def trunk(x, c, n_past, C0, n0, m0, buf0, w_ada, b_ada, g_mix, g_ffn, w_in, b_gates, g_head,
          w_pool, ls_pool, w_out, w_router, b_router, w_gate_up, b_gate_up, w_down, b_down, g_final):
    Cs, ns, ms, bufs = [], [], [], []
    for l in range(DEPTH):
        mod = jax.nn.silu(c) @ w_ada[l] + b_ada[l]
        sh1, sc1, gt1, sh2, sc2, gt2 = jnp.split(mod[:, None, :], N_MOD, axis=-1)
        h = rmsnorm(x, g_mix[l]) * (1 + sc1) + sh1
        y, C, n, m, buf = mixer(h, n_past, C0[l], n0[l], m0[l], buf0[l], w_in[l], b_gates[l], g_head[l],
                                w_pool[l], ls_pool[l], w_out[l])
        x = x + gt1 * y
        h = rmsnorm(x, g_ffn[l]) * (1 + sc2) + sh2
        x = x + gt2 * moe(h, w_router[l], b_router[l], w_gate_up[l], b_gate_up[l], w_down[l], b_down[l])
        Cs.append(C)
        ns.append(n)
        ms.append(m)
        bufs.append(buf)
    return rmsnorm(x, g_final), jnp.stack(Cs), jnp.stack(ns), jnp.stack(ms), jnp.stack(bufs)


def setup_inputs(seed: int = 0) -> dict:
    key = jax.random.key(seed)
    ks = jax.random.split(key, 26)
    f32 = jnp.float32

    def nrm(k, shape, scale):
        return jax.random.normal(k, shape, f32) * scale

    D = D_MODEL
    return {
        'x_prompt': nrm(ks[0], (BATCH, SEQ, D), 1.0),
        'x_sample': nrm(ks[1], (DEC_BATCH, DEC_SEQ, D), 1.0),
        'c_prompt': nrm(ks[2], (BATCH, D), 1.0),
        'c_sample': nrm(ks[3], (DEC_BATCH, D), 1.0),
        'state_mlstm_C': nrm(ks[4], (DEPTH, DEC_BATCH, ML_HEADS, ML_DK, ML_DV), 0.05),
        'state_mlstm_n': nrm(ks[5], (DEPTH, DEC_BATCH, ML_HEADS, ML_DK), 0.05),
        'state_mlstm_m': jax.random.uniform(ks[6], (DEPTH, DEC_BATCH, ML_HEADS), f32, 0.0, 3.0),
        'state_pool': nrm(ks[7], (DEPTH, DEC_BATCH, POOL_BUF, POOL_WIDTH), 1.0),
        'w_ada': nrm(ks[8], (DEPTH, D, N_MOD * D), 0.5 * D ** -0.5),
        'b_ada': nrm(ks[9], (DEPTH, N_MOD * D), 0.02),
        'g_mix': 1.0 + nrm(ks[10], (DEPTH, D), 0.1),
        'g_ffn': 1.0 + nrm(ks[11], (DEPTH, D), 0.1),
        'w_in': nrm(ks[12], (DEPTH, D, IN_COLS), D ** -0.5),
        'b_gates': jnp.concatenate([-1.0 + nrm(ks[13], (DEPTH, ML_HEADS), 0.1),
                                    3.0 + nrm(ks[14], (DEPTH, ML_HEADS), 0.5)], axis=-1),
        'g_head': 1.0 + nrm(ks[15], (DEPTH, ML_WIDTH), 0.1),
        'w_pool': nrm(ks[16], (DEPTH, POOL_GROUPS, POOL_GW, POOL_GW), POOL_GW ** -0.5),
        'ls_pool': 1.0 + nrm(ks[17], (DEPTH, POOL_WIDTH), 0.1),
        'w_out': nrm(ks[18], (DEPTH, MIX_WIDTH, D), MIX_WIDTH ** -0.5),
        'w_router': nrm(ks[19], (DEPTH, D, N_EXPERTS), D ** -0.5),
        'b_router': nrm(ks[20], (DEPTH, N_EXPERTS), 0.01),
        'w_gate_up': nrm(ks[21], (DEPTH, N_EXPERTS, D, 2 * D_FF), D ** -0.5),
        'b_gate_up': nrm(ks[22], (DEPTH, N_EXPERTS, 2 * D_FF), 0.02),
        'w_down': nrm(ks[23], (DEPTH, N_EXPERTS, D_FF, D), D_FF ** -0.5),
        'b_down': nrm(ks[24], (DEPTH, N_EXPERTS, D), 0.02),
        'g_final': 1.0 + nrm(ks[25], (D,), 0.1),
    }


def reference(x_prompt, x_sample, c_prompt, c_sample, state_mlstm_C, state_mlstm_n, state_mlstm_m,
              state_pool, w_ada, b_ada, g_mix, g_ffn, w_in, b_gates, g_head, w_pool, ls_pool, w_out,
              w_router, b_router, w_gate_up, b_gate_up, w_down, b_down, g_final):
    bp = x_prompt.shape[0]
    C0 = jnp.zeros((DEPTH, bp, ML_HEADS, ML_DK, ML_DV), jnp.float32)
    n0 = jnp.zeros((DEPTH, bp, ML_HEADS, ML_DK), jnp.float32)
    m0 = jnp.zeros((DEPTH, bp, ML_HEADS), jnp.float32)
    buf0 = jnp.zeros((DEPTH, bp, POOL_BUF, POOL_WIDTH), x_prompt.dtype)
    y_prompt, C_prompt, n_prompt, m_prompt, pool_prompt = trunk(
        x_prompt, c_prompt, 0, C0, n0, m0, buf0, w_ada, b_ada, g_mix, g_ffn, w_in, b_gates, g_head,
        w_pool, ls_pool, w_out, w_router, b_router, w_gate_up, b_gate_up, w_down, b_down, g_final)
    y_sample, C_sample, n_sample, m_sample, pool_sample = trunk(
        x_sample, c_sample, PAST_LEN, state_mlstm_C, state_mlstm_n, state_mlstm_m, state_pool,
        w_ada, b_ada, g_mix, g_ffn, w_in, b_gates, g_head, w_pool, ls_pool, w_out,
        w_router, b_router, w_gate_up, b_gate_up, w_down, b_down, g_final)
    return (y_prompt, y_sample, C_prompt, n_prompt, m_prompt, pool_prompt,
            C_sample, n_sample, m_sample, pool_sample)
```

```python
import functools

import jax
import jax.numpy as jnp
from jax import lax
from jax.experimental import pallas as pl
from jax.experimental.pallas import tpu as pltpu

F32 = jnp.float32
BF16 = jnp.bfloat16

D_MODEL = 2048
DEPTH = 2
ML_WIDTH = 1024
POOL_WIDTH = 1024
ML_HEADS = 4
ML_DK = 256
ML_DV = 256
POOL_WINDOWS = (2, 4, 8, 16)
POOL_GW = 256
POOL_BUF = 15
POOL_HALO = 16
N_GATES = 2 * ML_HEADS
UG_COLS = POOL_WIDTH + 128
N_EXPERTS = 32
ROUTER_LANES = 128
TOP_K = 4
D_FF = 2048
SWIGLU_LIMIT = 7.0
SWIGLU_ALPHA = 1.702
N_MOD = 6
EPS = 1e-5
PAST_LEN = 16384
NEG_BIG = -1e30

VMEM_LIMIT = 56 * 1024 * 1024
MOE_TM = 256
MLSTM_CHUNK_PROMPT = 256
SAMPLE_PAD_SEQ = 8
SAMPLE_POOL_STRIDE = 24


def _sigmoid(x):
    return 1.0 / (1.0 + jnp.exp(-x))


def _log_sigmoid(x):
    return jnp.minimum(x, 0.0) - jnp.log(1.0 + jnp.exp(-jnp.abs(x)))


def _params(*sem):
    return pltpu.CompilerParams(dimension_semantics=sem, vmem_limit_bytes=VMEM_LIMIT)


def _ada_kernel(c_ref, w_ref, b_ref, o_ref):
    c = c_ref[...]
    a = (c * _sigmoid(c)).astype(BF16)
    o_ref[...] = jnp.dot(a, w_ref[...].astype(BF16), preferred_element_type=F32) + b_ref[...]


def _ada(c_all, w_ada, b_ada):
    rows = c_all.shape[0]
    tn = 512
    ncol = N_MOD * D_MODEL
    return pl.pallas_call(
        _ada_kernel,
        grid=(DEPTH, ncol // tn),
        in_specs=[
            pl.BlockSpec((rows, D_MODEL), lambda l, n: (0, 0)),
            pl.BlockSpec((None, D_MODEL, tn), lambda l, n: (l, 0, n)),
            pl.BlockSpec((None, 1, tn), lambda l, n: (l, 0, n)),
        ],
        out_specs=pl.BlockSpec((None, rows, tn), lambda l, n: (l, 0, n)),
        out_shape=jax.ShapeDtypeStruct((DEPTH, rows, ncol), F32),
        compiler_params=_params("arbitrary", "arbitrary"),
        name="ada",
    )(c_all, w_ada, b_ada.reshape(DEPTH, 1, ncol))


def _normmod_kernel(x_ref, g_ref, sc_ref, sh_ref, o_ref):
    x = x_ref[...]
    y = x * lax.rsqrt(jnp.mean(x * x, axis=-1, keepdims=True) + EPS) * g_ref[...]
    o_ref[...] = (y * (1.0 + sc_ref[...]) + sh_ref[...]).astype(o_ref.dtype)


def _norm_kernel(x_ref, g_ref, o_ref):
    x = x_ref[...]
    o_ref[...] = x * lax.rsqrt(jnp.mean(x * x, axis=-1, keepdims=True) + EPS) * g_ref[...]


def _mod_spec(mod3, tiles_per_group, k, tn=D_MODEL, grid_rank=1):
    r = mod3.shape[1]
    nper = D_MODEL // tn
    if grid_rank == 1:
        return pl.BlockSpec((None, r, tn), lambda i: (i // tiles_per_group, 0, k))
    return pl.BlockSpec((None, r, tn), lambda n, i: (i // tiles_per_group, 0, k * nper + n))


def _normmod(x2, g, mod3, tiles_per_group, tm, k_scale, k_shift):
    m = x2.shape[0]
    return pl.pallas_call(
        _normmod_kernel,
        grid=(m // tm,),
        in_specs=[
            pl.BlockSpec((tm, D_MODEL), lambda i: (i, 0)),
            pl.BlockSpec((1, D_MODEL), lambda i: (0, 0)),
            _mod_spec(mod3, tiles_per_group, k_scale),
            _mod_spec(mod3, tiles_per_group, k_shift),
        ],
        out_specs=pl.BlockSpec((tm, D_MODEL), lambda i: (i, 0)),
        out_shape=jax.ShapeDtypeStruct((m, D_MODEL), BF16),
        compiler_params=_params("arbitrary"),
        name="normmod",
    )(x2, g.reshape(1, D_MODEL), mod3, mod3)


def _final_norm(x2, g, tm):
    m = x2.shape[0]
    return pl.pallas_call(
        _norm_kernel,
        grid=(m // tm,),
        in_specs=[
            pl.BlockSpec((tm, D_MODEL), lambda i: (i, 0)),
            pl.BlockSpec((1, D_MODEL), lambda i: (0, 0)),
        ],
        out_specs=pl.BlockSpec((tm, D_MODEL), lambda i: (i, 0)),
        out_shape=jax.ShapeDtypeStruct((m, D_MODEL), F32),
        compiler_params=_params("arbitrary"),
        name="final_norm",
    )(x2, g.reshape(1, D_MODEL))


def _mm_kernel(a_ref, w_ref, o_ref, wbf_ref):
    @pl.when(pl.program_id(1) == 0)
    def _():
        wbf_ref[...] = w_ref[...].astype(BF16)

    o_ref[...] = jnp.dot(a_ref[...].astype(BF16), wbf_ref[...], preferred_element_type=F32)


def _mm_resid_kernel(a_ref, w_ref, x_ref, gt_ref, o_ref, wbf_ref):
    @pl.when(pl.program_id(1) == 0)
    def _():
        wbf_ref[...] = w_ref[...].astype(BF16)

    y = jnp.dot(a_ref[...].astype(BF16), wbf_ref[...], preferred_element_type=F32)
    o_ref[...] = x_ref[...] + gt_ref[...] * y


def _mm(a, w, w_spec, n_out, tm, tn):
    m, k = a.shape
    return pl.pallas_call(
        _mm_kernel,
        grid=(n_out // tn, m // tm),
        in_specs=[pl.BlockSpec((tm, k), lambda n, i: (i, 0)), w_spec],
        out_specs=pl.BlockSpec((tm, tn), lambda n, i: (i, n)),
        out_shape=jax.ShapeDtypeStruct((m, n_out), F32),
        scratch_shapes=[pltpu.VMEM((k, tn), BF16)],
        compiler_params=_params("arbitrary", "arbitrary"),
        name="mm",
    )(a, w)


def _mm_resid(a, w, w_spec, x2, mod3, tiles_per_group, k_gate, tm, tn):
    m, k = a.shape
    return pl.pallas_call(
        _mm_resid_kernel,
        grid=(D_MODEL // tn, m // tm),
        in_specs=[
            pl.BlockSpec((tm, k), lambda n, i: (i, 0)),
            w_spec,
            pl.BlockSpec((tm, tn), lambda n, i: (i, n)),
            _mod_spec(mod3, tiles_per_group, k_gate, tn=tn, grid_rank=2),
        ],
        out_specs=pl.BlockSpec((tm, tn), lambda n, i: (i, n)),
        out_shape=jax.ShapeDtypeStruct((m, D_MODEL), F32),
        scratch_shapes=[pltpu.VMEM((k, tn), BF16)],
        compiler_params=_params("arbitrary", "arbitrary"),
        name="mm_resid",
    )(a, w, x2, mod3)


def _mlstm_kernel(q_ref, k_ref, v_ref, og_ref, gc_ref, gr_ref, gh_ref, c0_ref, n0_ref, m0_ref,
                  hm_ref, c_ref, n_ref, m_ref, *, chunk, valid, bblk):
    head = pl.program_id(1)

    @pl.when(pl.program_id(2) == 0)
    def _():
        c_ref[...] = c0_ref[...]
        n_ref[...] = n0_ref[...]
        m_ref[...] = m0_ref[...]

    L = chunk
    t_row = lax.broadcasted_iota(jnp.int32, (L, L), 0)
    t_col = lax.broadcasted_iota(jnp.int32, (L, L), 1)
    causal = t_col <= t_row
    lane8 = lax.broadcasted_iota(jnp.int32, (L, N_GATES), 1)
    sub8 = lax.broadcasted_iota(jnp.int32, (N_GATES, L), 0)
    pos_c = lax.broadcasted_iota(jnp.int32, (L, 1), 0)
    pos_r = lax.broadcasted_iota(jnp.int32, (1, L), 1)

    for bb in range(bblk):
        gc = gc_ref[bb]
        gr = gr_ref[bb]
        ig_c = jnp.sum(jnp.where(lane8 == head, gc, 0.0), axis=1, keepdims=True)
        fg_c = jnp.sum(jnp.where(lane8 == head + ML_HEADS, gc, 0.0), axis=1, keepdims=True)
        ig_r = jnp.sum(jnp.where(sub8 == head, gr, 0.0), axis=0, keepdims=True)
        fg_r = jnp.sum(jnp.where(sub8 == head + ML_HEADS, gr, 0.0), axis=0, keepdims=True)
        lf_c = _log_sigmoid(fg_c)
        lf_r = _log_sigmoid(fg_r)
        if valid < L:
            ig_c = jnp.where(pos_c < valid, ig_c, -jnp.inf)
            ig_r = jnp.where(pos_r < valid, ig_r, -jnp.inf)
            lf_c = jnp.where(pos_c < valid, lf_c, 0.0)
            lf_r = jnp.where(pos_r < valid, lf_r, 0.0)
        b_c = jnp.sum(jnp.where(causal, lf_r, 0.0), axis=1, keepdims=True)
        b_r = jnp.sum(jnp.where(t_row <= t_col, lf_c, 0.0), axis=0, keepdims=True)
        b_end = jnp.sum(lf_r, axis=1, keepdims=True)
        m_prev = m_ref[bb][:, 0:1]

        logw = jnp.where(causal, b_c - b_r + ig_r, -jnp.inf)
        inter = b_c + m_prev
        m_t = jnp.maximum(inter, jnp.max(logw, axis=1, keepdims=True))

        q = q_ref[bb]
        k = k_ref[bb] * (ML_DK ** -0.5)
        qb = q.astype(BF16)
        kb = k.astype(BF16)
        vb = v_ref[bb].astype(BF16)
        c_prev = c_ref[bb]
        n_prev = n_ref[bb]

        qk = lax.dot_general(qb, kb, (((1,), (1,)), ((), ())), preferred_element_type=F32)
        s = qk * jnp.exp(logw - m_t)
        a = jnp.exp(inter - m_t)
        num = a * jnp.dot(qb, c_prev.astype(BF16), preferred_element_type=F32) \
            + jnp.dot(s.astype(BF16), vb, preferred_element_type=F32)
        den = a * jnp.sum(q * n_prev, axis=1, keepdims=True) + jnp.sum(s, axis=1, keepdims=True)
        hc = num / jnp.maximum(jnp.abs(den), jnp.exp(-m_t))
        hn = hc * lax.rsqrt(jnp.mean(hc * hc, axis=1, keepdims=True) + EPS) * gh_ref[...]
        hm_ref[bb] = (hn * _sigmoid(og_ref[bb])).astype(hm_ref.dtype)

        logw_end = b_end - b_c + ig_c
        m_new = jnp.maximum(b_end + m_prev, jnp.max(logw_end, axis=0, keepdims=True))
        decay = jnp.exp(b_end + m_prev - m_new)
        wk = k * jnp.exp(logw_end - m_new)
        c_ref[bb] = decay * c_prev + lax.dot_general(
            wk.astype(BF16), vb, (((0,), (0,)), ((), ())), preferred_element_type=F32)
        n_ref[bb] = decay * n_prev + jnp.sum(wk, axis=0, keepdims=True)
        m_ref[bb] = jnp.broadcast_to(m_new, (1, 128))


def _mlstm(proj3, gates3, g_head, c0, n0, m0, chunk, valid, bblk):
    b, s, _ = proj3.shape
    h = ML_HEADS
    gates_t = jnp.swapaxes(gates3, 1, 2)
    n0 = n0.reshape(b, h, 1, ML_DK)
    m0 = jnp.broadcast_to(m0.reshape(b, h, 1, 1), (b, h, 1, 128))

    def col(off):
        return pl.BlockSpec((bblk, chunk, ML_DK), lambda bi, hi, ci: (bi, ci, off * h + hi))

    state_specs = [
        pl.BlockSpec((bblk, None, ML_DK, ML_DV), lambda bi, hi, ci: (bi, hi, 0, 0)),
        pl.BlockSpec((bblk, None, 1, ML_DK), lambda bi, hi, ci: (bi, hi, 0, 0)),
        pl.BlockSpec((bblk, None, 1, 128), lambda bi, hi, ci: (bi, hi, 0, 0)),
    ]
    hm, c, n, m = pl.pallas_call(
        functools.partial(_mlstm_kernel, chunk=chunk, valid=valid, bblk=bblk),
        grid=(b // bblk, h, s // chunk),
        in_specs=[
            col(0), col(1), col(2), col(3),
            pl.BlockSpec((bblk, chunk, N_GATES), lambda bi, hi, ci: (bi, ci, 0)),
            pl.BlockSpec((bblk, N_GATES, chunk), lambda bi, hi, ci: (bi, 0, ci)),
            pl.BlockSpec((None, 1, ML_DV), lambda bi, hi, ci: (hi, 0, 0)),
        ] + state_specs,
        out_specs=[pl.BlockSpec((bblk, chunk, ML_DV), lambda bi, hi, ci: (bi, ci, hi))] + state_specs,
        out_shape=[
            jax.ShapeDtypeStruct((b, s, ML_WIDTH), F32),
            jax.ShapeDtypeStruct((b, h, ML_DK, ML_DV), F32),
            jax.ShapeDtypeStruct((b, h, 1, ML_DK), F32),
            jax.ShapeDtypeStruct((b, h, 1, 128), F32),
        ],
        compiler_params=_params("arbitrary", "arbitrary", "arbitrary"),
        name="mlstm",
    )(proj3, proj3, proj3, proj3, gates3, gates_t, g_head.reshape(h, 1, ML_DV), c0, n0, m0)
    return hm, c, n.reshape(b, h, ML_DK), m[:, :, 0, 0]


def _pool_kernel(u_ref, wp_ref, ls_ref, p_ref, ext_ref, *, tile, n_past):
    i = pl.program_id(1)

    @pl.when(i == 0)
    def _():
        ext_ref[0:POOL_HALO, :] = jnp.zeros((POOL_HALO, POOL_WIDTH), F32)

    ext_ref[POOL_HALO:POOL_HALO + tile, :] = u_ref[...]
    t = i * tile + lax.broadcasted_iota(jnp.int32, (tile, 1), 0)
    for g, w in enumerate(POOL_WINDOWS):
        cols = slice(g * POOL_GW, (g + 1) * POOL_GW)
        e = ext_ref[:, cols]
        win = e
        sh = 1
        while sh < w:
            win = win + pltpu.roll(win, sh, 0)
            sh *= 2
        cnt = jnp.minimum(n_past + t + 1, w).astype(F32)
        d = win[POOL_HALO:] / cnt - e[POOL_HALO:]
        p = jnp.dot(d.astype(BF16), wp_ref[g].astype(BF16), preferred_element_type=F32)
        p_ref[:, cols] = p * ls_ref[:, cols]
    ext_ref[0:POOL_HALO, :] = ext_ref[tile:tile + POOL_HALO, :]


def _pool(u3, w_pool_l, ls_pool_l, tile, n_past):
    b, s, _ = u3.shape
    return pl.pallas_call(
        functools.partial(_pool_kernel, tile=tile, n_past=n_past),
        grid=(b, s // tile),
        in_specs=[
            pl.BlockSpec((None, tile, POOL_WIDTH), lambda bi, i: (bi, i, 0)),
            pl.BlockSpec((len(POOL_WINDOWS), POOL_GW, POOL_GW), lambda bi, i: (0, 0, 0)),
            pl.BlockSpec((1, POOL_WIDTH), lambda bi, i: (0, 0)),
        ],
        out_specs=pl.BlockSpec((None, tile, POOL_WIDTH), lambda bi, i: (bi, i, 0)),
        out_shape=jax.ShapeDtypeStruct((b, s, POOL_WIDTH), F32),
        scratch_shapes=[pltpu.VMEM((POOL_HALO + tile, POOL_WIDTH), F32)],
        compiler_params=_params("arbitrary", "arbitrary"),
        name="pool",
    )(u3, w_pool_l, ls_pool_l.reshape(1, POOL_WIDTH))


def _router_kernel(h_ref, w_ref, b_ref, idx_ref, gate_ref):
    tm = h_ref.shape[0]
    logits = jnp.dot(h_ref[...], w_ref[...].astype(BF16), preferred_element_type=F32) + b_ref[...]
    lane = lax.broadcasted_iota(jnp.int32, (tm, ROUTER_LANES), 1)
    lane_f = lane.astype(F32)
    vals, idxs = [], []
    for _ in range(TOP_K):
        mx = jnp.max(logits, axis=1, keepdims=True)
        ix = jnp.min(jnp.where(logits == mx, lane_f, float(ROUTER_LANES)), axis=1, keepdims=True)
        vals.append(mx)
        idxs.append(ix)
        logits = jnp.where(lane_f == ix, -jnp.inf, logits)
    ex = [jnp.exp(v - vals[0]) for v in vals]
    tot = ex[0] + ex[1] + ex[2] + ex[3]
    gate = jnp.zeros((tm, ROUTER_LANES), F32)
    idx = jnp.zeros((tm, ROUTER_LANES), F32)
    for k in range(TOP_K):
        gate = jnp.where(lane == k, ex[k] / tot, gate)
        idx = jnp.where(lane == k, idxs[k], idx)
    gate_ref[...] = gate
    idx_ref[...] = idx.astype(jnp.int32)


def _router(h2, w_router_l, b_router_l, tm):
    n = h2.shape[0]
    w = jnp.pad(w_router_l, ((0, 0), (0, ROUTER_LANES - N_EXPERTS)))
    b = jnp.pad(b_router_l, (0, ROUTER_LANES - N_EXPERTS), constant_values=NEG_BIG)
    idx, gate = pl.pallas_call(
        _router_kernel,
        grid=(n // tm,),
        in_specs=[
            pl.BlockSpec((tm, D_MODEL), lambda i: (i, 0)),
            pl.BlockSpec((D_MODEL, ROUTER_LANES), lambda i: (0, 0)),
            pl.BlockSpec((1, ROUTER_LANES), lambda i: (0, 0)),
        ],
        out_specs=[pl.BlockSpec((tm, ROUTER_LANES), lambda i: (i, 0))] * 2,
        out_shape=[jax.ShapeDtypeStruct((n, ROUTER_LANES), jnp.int32),
                   jax.ShapeDtypeStruct((n, ROUTER_LANES), F32)],
        compiler_params=_params("arbitrary"),
        name="router",
    )(h2, w, b.reshape(1, ROUTER_LANES))
    return idx[:, :TOP_K], gate[:, :TOP_K]


def _gate_up_kernel(be_ref, nused_ref, x_ref, wg_ref, wu_ref, bg_ref, bu_ref, act_ref, wgb_ref, wub_ref):
    j = pl.program_id(1)
    prev = be_ref[jnp.maximum(j - 1, 0)]

    @pl.when((j == 0) | (be_ref[j] != prev))
    def _():
        wgb_ref[...] = wg_ref[...].astype(BF16)
        wub_ref[...] = wu_ref[...].astype(BF16)

    @pl.when(j < nused_ref[0])
    def _():
        x = x_ref[...]
        g = jnp.dot(x, wgb_ref[...], preferred_element_type=F32) + bg_ref[...]
        u = jnp.dot(x, wub_ref[...], preferred_element_type=F32) + bu_ref[...]
        g = jnp.minimum(g, SWIGLU_LIMIT)
        u = jnp.clip(u, -SWIGLU_LIMIT, SWIGLU_LIMIT)
        act_ref[...] = (g * _sigmoid(SWIGLU_ALPHA * g) * (u + 1.0)).astype(act_ref.dtype)

    @pl.when(j >= nused_ref[0])
    def _():
        act_ref[...] = jnp.zeros(act_ref.shape, act_ref.dtype)


def _gate_up(x_sorted, block_expert, n_used, w_gate_up, b_gate_up, layer, tn):
    a_pad = x_sorted.shape[0]
    nt = D_FF // tn
    b4 = b_gate_up.reshape(DEPTH, N_EXPERTS, 1, 2 * D_FF)
    grid_spec = pltpu.PrefetchScalarGridSpec(
        num_scalar_prefetch=2,
        grid=(nt, a_pad // MOE_TM),
        in_specs=[
            pl.BlockSpec((MOE_TM, D_MODEL), lambda n, j, be, nu: (j, 0)),
            pl.BlockSpec((None, None, D_MODEL, tn), lambda n, j, be, nu: (layer, be[j], 0, n)),
            pl.BlockSpec((None, None, D_MODEL, tn), lambda n, j, be, nu: (layer, be[j], 0, n + nt)),
            pl.BlockSpec((None, None, 1, tn), lambda n, j, be, nu: (layer, be[j], 0, n)),
            pl.BlockSpec((None, None, 1, tn), lambda n, j, be, nu: (layer, be[j], 0, n + nt)),
        ],
        out_specs=pl.BlockSpec((MOE_TM, tn), lambda n, j, be, nu: (j, n)),
        scratch_shapes=[pltpu.VMEM((D_MODEL, tn), BF16), pltpu.VMEM((D_MODEL, tn), BF16)],
    )
    return pl.pallas_call(
        _gate_up_kernel,
        grid_spec=grid_spec,
        out_shape=jax.ShapeDtypeStruct((a_pad, D_FF), BF16),
        compiler_params=_params("arbitrary", "arbitrary"),
        name="gate_up",
    )(block_expert, n_used, x_sorted, w_gate_up, w_gate_up, b4, b4)


def _down_kernel(be_ref, nused_ref, a_ref, w_ref, b_ref, rg_ref, y_ref, wb_ref):
    j = pl.program_id(1)
    prev = be_ref[jnp.maximum(j - 1, 0)]

    @pl.when((j == 0) | (be_ref[j] != prev))
    def _():
        wb_ref[...] = w_ref[...].astype(BF16)

    @pl.when(j < nused_ref[0])
    def _():
        y = jnp.dot(a_ref[...], wb_ref[...], preferred_element_type=F32) + b_ref[...]
        y_ref[...] = y * rg_ref[...]

    @pl.when(j >= nused_ref[0])
    def _():
        y_ref[...] = jnp.zeros(y_ref.shape, y_ref.dtype)


def _down(act, block_expert, n_used, row_gate, w_down, b_down, layer, tn):
    a_pad = act.shape[0]
    b4 = b_down.reshape(DEPTH, N_EXPERTS, 1, D_MODEL)
    grid_spec = pltpu.PrefetchScalarGridSpec(
        num_scalar_prefetch=2,
        grid=(D_MODEL // tn, a_pad // MOE_TM),
        in_specs=[
            pl.BlockSpec((MOE_TM, D_FF), lambda n, j, be, nu: (j, 0)),
            pl.BlockSpec((None, None, D_FF, tn), lambda n, j, be, nu: (layer, be[j], 0, n)),
            pl.BlockSpec((None, None, 1, tn), lambda n, j, be, nu: (layer, be[j], 0, n)),
            pl.BlockSpec((MOE_TM, 1), lambda n, j, be, nu: (j, 0)),
        ],
        out_specs=pl.BlockSpec((MOE_TM, tn), lambda n, j, be, nu: (j, n)),
        scratch_shapes=[pltpu.VMEM((D_FF, tn), BF16)],
    )
    return pl.pallas_call(
        _down_kernel,
        grid_spec=grid_spec,
        out_shape=jax.ShapeDtypeStruct((a_pad, D_MODEL), F32),
        compiler_params=_params("arbitrary", "arbitrary"),
        name="down",
    )(block_expert, n_used, act, w_down, b4, row_gate)


def _combine_kernel(x_ref, y4_ref, gt_ref, o_ref):
    y = y4_ref[:, 0:D_MODEL]
    for k in range(1, TOP_K):
        y = y + y4_ref[:, k * D_MODEL:(k + 1) * D_MODEL]
    o_ref[...] = x_ref[...] + gt_ref[...] * y


def _combine(x2, y4, mod3, tiles_per_group, k_gate, tm):
    m = x2.shape[0]
    return pl.pallas_call(
        _combine_kernel,
        grid=(m // tm,),
        in_specs=[
            pl.BlockSpec((tm, D_MODEL), lambda i: (i, 0)),
            pl.BlockSpec((tm, TOP_K * D_MODEL), lambda i: (i, 0)),
            _mod_spec(mod3, tiles_per_group, k_gate),
        ],
        out_specs=pl.BlockSpec((tm, D_MODEL), lambda i: (i, 0)),
        out_shape=jax.ShapeDtypeStruct((m, D_MODEL), F32),
        compiler_params=_params("arbitrary"),
        name="combine",
    )(x2, y4, mod3)


def _dispatch_plan(top_idx):
    n = top_idx.shape[0]
    a = n * TOP_K
    n_blocks = -(-(a + N_EXPERTS * (MOE_TM - 1)) // MOE_TM)
    e_flat = top_idx.reshape(a)
    onehot = (e_flat[:, None] == jnp.arange(N_EXPERTS, dtype=jnp.int32)[None, :]).astype(jnp.int32)
    csum = jnp.cumsum(onehot, axis=0)
    rank = jnp.sum(onehot * csum, axis=1) - 1
    counts = csum[-1]
    padded = (counts + MOE_TM - 1) // MOE_TM * MOE_TM
    pend = jnp.cumsum(padded)
    pstart = pend - padded
    dest = (pstart[e_flat] + rank).astype(jnp.int32)
    row_assign = jnp.full((n_blocks * MOE_TM,), -1, jnp.int32).at[dest].set(jnp.arange(a, dtype=jnp.int32))
    block_expert = jnp.minimum(
        jnp.searchsorted(pend, jnp.arange(n_blocks, dtype=jnp.int32) * MOE_TM, side='right'),
        N_EXPERTS - 1).astype(jnp.int32)
    n_used = (pend[-1] // MOE_TM).astype(jnp.int32).reshape(1)
    return dest, row_assign, block_expert, n_used


def _moe(h2, w_router, b_router, w_gate_up, b_gate_up, w_down, b_down, layer):
    n = h2.shape[0]
    top_idx, gates = _router(h2, w_router[layer], b_router[layer], tm=256)
    dest, row_assign, block_expert, n_used = _dispatch_plan(top_idx)
    valid = row_assign >= 0
    safe = jnp.maximum(row_assign, 0)
    x_sorted = h2[safe // TOP_K]
    row_gate = jnp.where(valid, gates.reshape(-1)[safe], 0.0).reshape(-1, 1)
    act = _gate_up(x_sorted, block_expert, n_used, w_gate_up, b_gate_up, layer, tn=512)
    y_pad = _down(act, block_expert, n_used, row_gate, w_down, b_down, layer, tn=1024)
    return y_pad[dest].reshape(n, TOP_K * D_MODEL)


def kernel(x_prompt, x_sample, c_prompt, c_sample, state_mlstm_C, state_mlstm_n, state_mlstm_m, state_pool,
           w_ada, b_ada, g_mix, g_ffn, w_in, b_gates, g_head, w_pool, ls_pool, w_out, w_router, b_router,
           w_gate_up, b_gate_up, w_down, b_down, g_final):
    bp, sp, _ = x_prompt.shape
    bs, ss, _ = x_sample.shape
    np_tok = bp * sp
    ns_tok = bs * ss

    c_all = jnp.concatenate([c_prompt, c_sample, jnp.zeros((4, D_MODEL), F32)], axis=0)
    mod = _ada(c_all, w_ada, b_ada)

    xp = x_prompt.reshape(np_tok, D_MODEL)
    xs = x_sample.reshape(ns_tok, D_MODEL)
    tm_p, tm_s, tm_c = 512, 256, 256
    tpg_p = sp // tm_p

    zc = jnp.zeros((bp, ML_HEADS, ML_DK, ML_DV), F32)
    zn = jnp.zeros((bp, ML_HEADS, ML_DK), F32)
    zm = jnp.zeros((bp, ML_HEADS), F32)

    outs_p = {"C": [], "n": [], "m": [], "buf": []}
    outs_s = {"C": [], "n": [], "m": [], "buf": []}

    for l in range(DEPTH):
        mod_p = mod[l, :bp].reshape(bp, 1, N_MOD * D_MODEL)
        mod_s = jnp.repeat(mod[l, bp:bp + bs], ss, axis=0).reshape(ns_tok // tm_s, tm_s, N_MOD * D_MODEL)
        w_ug = jnp.concatenate([
            w_in[l][:, 4 * ML_WIDTH + N_GATES:],
            w_in[l][:, 4 * ML_WIDTH:4 * ML_WIDTH + N_GATES],
            jnp.zeros((D_MODEL, UG_COLS - POOL_WIDTH - N_GATES), F32)], axis=1)
        w_in_spec = pl.BlockSpec((None, D_MODEL, 1024), lambda n, i: (l, 0, n))
        w_ug_spec = pl.BlockSpec((D_MODEL, UG_COLS), lambda n, i: (0, 0))
        w_out_spec = pl.BlockSpec((None, D_MODEL, 1024), lambda n, i: (l, 0, n))

        hp = _normmod(xp, g_mix[l], mod_p, tpg_p, tm_p, 1, 0)
        proj_p = _mm(hp, w_in, w_in_spec, 4 * ML_WIDTH, tm_p, 1024)
        ug_p = _mm(hp, w_ug, w_ug_spec, UG_COLS, tm_p, UG_COLS)
        ug_p3 = ug_p.reshape(bp, sp, UG_COLS)
        gates_p = ug_p3[:, :, POOL_WIDTH:POOL_WIDTH + N_GATES] + b_gates[l]
        hm_p, c_p, n_p, m_p = _mlstm(proj_p.reshape(bp, sp, 4 * ML_WIDTH), gates_p, g_head[l], zc, zn, zm,
                                     chunk=MLSTM_CHUNK_PROMPT, valid=MLSTM_CHUNK_PROMPT, bblk=1)
        pool_p = _pool(ug_p3, w_pool[l], ls_pool[l], tile=512, n_past=0)
        mix_p = jnp.concatenate([hm_p, pool_p], axis=-1).reshape(np_tok, D_MODEL)
        xp = _mm_resid(mix_p, w_out, w_out_spec, xp, mod_p, tpg_p, 2, tm_p, 1024)
        outs_p["C"].append(c_p)
        outs_p["n"].append(n_p)
        outs_p["m"].append(m_p)
        outs_p["buf"].append(ug_p3[:, sp - POOL_BUF:, :POOL_WIDTH])

        hs = _normmod(xs, g_mix[l], mod_s, 1, tm_s, 1, 0)
        proj_s = _mm(hs, w_in, w_in_spec, 4 * ML_WIDTH, tm_s, 1024)
        ug_s = _mm(hs, w_ug, w_ug_spec, UG_COLS, tm_s, UG_COLS)
        ug_s3 = ug_s.reshape(bs, ss, UG_COLS)
        u_s = ug_s3[:, :, :POOL_WIDTH]
        gates_s = ug_s3[:, :, POOL_WIDTH:POOL_WIDTH + N_GATES] + b_gates[l]
        pad_t = SAMPLE_PAD_SEQ - ss
        proj_s3 = jnp.pad(proj_s.reshape(bs, ss, 4 * ML_WIDTH), ((0, 0), (0, pad_t), (0, 0)))
        gates_s = jnp.pad(gates_s, ((0, 0), (0, pad_t), (0, 0)))
        hm_s, c_s, n_s, m_s = _mlstm(proj_s3, gates_s, g_head[l], state_mlstm_C[l], state_mlstm_n[l],
                                     state_mlstm_m[l], chunk=SAMPLE_PAD_SEQ, valid=ss, bblk=8)
        hist = jnp.concatenate([state_pool[l], u_s], axis=1)
        seq = jnp.pad(hist, ((0, 0), (1, SAMPLE_POOL_STRIDE - 1 - POOL_BUF - ss), (0, 0)))
        pool_s = _pool(seq.reshape(1, bs * SAMPLE_POOL_STRIDE, POOL_WIDTH), w_pool[l], ls_pool[l],
                       tile=512, n_past=PAST_LEN)
        pool_s = pool_s.reshape(bs, SAMPLE_POOL_STRIDE, POOL_WIDTH)[:, POOL_HALO:POOL_HALO + ss]
        mix_s = jnp.concatenate([hm_s[:, :ss], pool_s], axis=-1).reshape(ns_tok, D_MODEL)
        xs = _mm_resid(mix_s, w_out, w_out_spec, xs, mod_s, 1, 2, tm_s, 1024)
        outs_s["C"].append(c_s)
        outs_s["n"].append(n_s)
        outs_s["m"].append(m_s)
        outs_s["buf"].append(hist[:, -POOL_BUF:])

        hp2 = _normmod(xp, g_ffn[l], mod_p, tpg_p, tm_p, 4, 3)
        hs2 = _normmod(xs, g_ffn[l], mod_s, 1, tm_s, 4, 3)
        y4 = _moe(jnp.concatenate([hp2, hs2], axis=0), w_router, b_router, w_gate_up, b_gate_up,
                  w_down, b_down, l)
        xp = _combine(xp, y4[:np_tok], mod_p, sp // tm_c, 5, tm_c)
        xs = _combine(xs, y4[np_tok:], mod_s, 1, 5, tm_s)

    y_prompt = _final_norm(xp, g_final, tm_p).reshape(bp, sp, D_MODEL)
    y_sample = _final_norm(xs, g_final, tm_s).reshape(bs, ss, D_MODEL)
    return (y_prompt, y_sample,
            jnp.stack(outs_p["C"]), jnp.stack(outs_p["n"]), jnp.stack(outs_p["m"]), jnp.stack(outs_p["buf"]),
            jnp.stack(outs_s["C"]), jnp.stack(outs_s["n"]), jnp.stack(outs_s["m"]), jnp.stack(outs_s["buf"]))
```
